```python
import jax, jax.numpy as jnp
from jax import lax
import numpy as np

D_MODEL = 1024
BATCH = 8
SEQ = 4096
DEPTH = 1

CHUNK = 64
SUB = 16
N_SUB = CHUNK // SUB
HG_DK = 128
HG_HEADS = D_MODEL // HG_DK
HG_DV = D_MODEL // HG_HEADS
HG_WIDTH = HG_HEADS * HG_DK
HG_VWIDTH = HG_HEADS * HG_DV
CONV_WIDTH = D_MODEL
CONV_TAPS = 31
FFN_HIDDEN = -(-8 * D_MODEL // (3 * 256)) * 256
DEEPNORM_ALPHA = (2 * DEPTH) ** 0.25
DEEPNORM_BETA = (8 * DEPTH) ** -0.25
LN_EPS = 1e-5
IN_WIDTH = 2 * HG_WIDTH + 2 * HG_VWIDTH + 2 * CONV_WIDTH + 2 * D_MODEL
IN_SPLITS = [HG_WIDTH,
             2 * HG_WIDTH,
             2 * HG_WIDTH + HG_VWIDTH,
             2 * HG_WIDTH + 2 * HG_VWIDTH,
             2 * HG_WIDTH + 2 * HG_VWIDTH + CONV_WIDTH,
             2 * HG_WIDTH + 2 * HG_VWIDTH + 2 * CONV_WIDTH,
             2 * HG_WIDTH + 2 * HG_VWIDTH + 2 * CONV_WIDTH + D_MODEL]

kernel_name = "hgrn2_conformer_conv_gated_hybrid"


def layer_norm(x, g, b):
    xf = x.astype(jnp.float32)
    mu = jnp.mean(xf, axis=-1, keepdims=True)
    var = jnp.mean(jnp.square(xf - mu), axis=-1, keepdims=True)
    y = (xf - mu) * lax.rsqrt(var + LN_EPS) * g.astype(jnp.float32) + b.astype(jnp.float32)
    return y.astype(x.dtype)


def rms_norm(x, g):
    xf = x.astype(jnp.float32)
    y = xf * lax.rsqrt(jnp.mean(jnp.square(xf), axis=-1, keepdims=True) + LN_EPS)
    return y * g.astype(jnp.float32)


def _hgrn2_chunk_step(state, inp):
    q, k, v, g = inp
    bsz, h = q.shape[0], q.shape[1]
    b = jnp.cumsum(g, axis=2)
    o = jnp.einsum('bhck,bhkv->bhcv', q * jnp.exp(b), state)
    qs = q.reshape(bsz, h, N_SUB, SUB, HG_DK)
    ks = k.reshape(bsz, h, N_SUB, SUB, HG_DK)
    vs = v.reshape(bsz, h, N_SUB, SUB, HG_DV)
    bs = b.reshape(bsz, h, N_SUB, SUB, HG_DK)
    tri = jnp.tril(jnp.ones((SUB, SUB), dtype=bool))
    diff = bs[:, :, :, :, None, :] - bs[:, :, :, None, :, :]
    decay = jnp.exp(jnp.where(tri[:, :, None], diff, -jnp.inf))
    a_diag = jnp.einsum('bhntk,bhnsk,bhntsk->bhnts', qs, ks, decay)
    o_diag = jnp.einsum('bhnts,bhnsv->bhntv', a_diag, vs)
    ref = jnp.concatenate([jnp.zeros_like(bs[:, :, :1, -1]), bs[:, :, :-1, -1]], axis=2)
    q_x = qs * jnp.exp(bs - ref[:, :, :, None, :])
    earlier = jnp.arange(CHUNK)[None, :] < (jnp.arange(N_SUB) * SUB)[:, None]
    k_exp = ref[:, :, :, None, :] - b[:, :, None, :, :]
    k_x = k[:, :, None] * jnp.exp(jnp.where(earlier[:, :, None], k_exp, -jnp.inf))
    a_cross = jnp.einsum('bhntk,bhnsk->bhnts', q_x, k_x)
    o_cross = jnp.einsum('bhnts,bhsv->bhntv', a_cross, v)
    o = o + (o_diag + o_cross).reshape(bsz, h, CHUNK, HG_DV)
    b_end = b[:, :, -1]
    k_end = k * jnp.exp(b_end[:, :, None, :] - b)
    state = jnp.exp(b_end)[..., None] * state + jnp.einsum('bhck,bhcv->bhkv', k_end, v)
    return state, o


def hgrn2_recurrence(q, k, v, g):
    bsz, seq = q.shape[0], q.shape[1]
    n = seq // CHUNK

    def to_chunks(t):
        return t.reshape(bsz, n, CHUNK, HG_HEADS, t.shape[-1]).transpose(1, 0, 3, 2, 4)

    state0 = jnp.zeros((bsz, HG_HEADS, HG_DK, HG_DV), jnp.float32)
    _, o = lax.scan(_hgrn2_chunk_step, state0, (to_chunks(q), to_chunks(k), to_chunks(v), to_chunks(g)))
    return o.transpose(1, 0, 3, 2, 4).reshape(bsz, seq, HG_HEADS, HG_DV)


def causal_depthwise_conv(u, w, bias):
    y = lax.conv_general_dilated(u, w[:, None, :].astype(u.dtype), window_strides=(1,),
                                 padding=[(CONV_TAPS - 1, 0)],
                                 dimension_numbers=('NWC', 'WIO', 'NWC'),
                                 feature_group_count=u.shape[-1])
    return y + bias.astype(u.dtype)


def setup_inputs(seed: int = 0) -> dict:
    key = jax.random.key(seed)
    ks = jax.random.split(key, 20)
    f32 = jnp.float32

    def nrm(k, shape, scale):
        return jax.random.normal(k, shape, f32) * scale

    return {
        "x": jax.random.normal(ks[0], (BATCH, SEQ, D_MODEL), f32),
        "w_in": nrm(ks[1], (DEPTH, D_MODEL, IN_WIDTH), D_MODEL ** -0.5),
        "lb_param": nrm(ks[2], (DEPTH + 1, HG_WIDTH), 0.1),
        "hg_norm_g": 1.0 + nrm(ks[3], (DEPTH, HG_DV), 0.01),
        "w_hg_out": nrm(ks[4], (DEPTH, HG_VWIDTH, D_MODEL), HG_VWIDTH ** -0.5),
        "conv_w": nrm(ks[5], (DEPTH, CONV_TAPS, CONV_WIDTH), CONV_TAPS ** -0.5),
        "conv_b": nrm(ks[6], (DEPTH, CONV_WIDTH), 0.01),
        "conv_ln_g": 1.0 + nrm(ks[7], (DEPTH, CONV_WIDTH), 0.01),
        "conv_ln_b": nrm(ks[8], (DEPTH, CONV_WIDTH), 0.01),
        "w_conv_out": nrm(ks[9], (DEPTH, CONV_WIDTH, D_MODEL), CONV_WIDTH ** -0.5),
        "w_out": nrm(ks[10], (DEPTH, D_MODEL, D_MODEL), D_MODEL ** -0.5 * DEEPNORM_BETA),
        "ln1_g": 1.0 + nrm(ks[11], (DEPTH, D_MODEL), 0.01),
        "ln1_b": nrm(ks[12], (DEPTH, D_MODEL), 0.01),
        "w_ffn_in": nrm(ks[13], (DEPTH, D_MODEL, 2 * FFN_HIDDEN), D_MODEL ** -0.5),
        "w_ffn_out": nrm(ks[14], (DEPTH, FFN_HIDDEN, D_MODEL), FFN_HIDDEN ** -0.5 * DEEPNORM_BETA),
        "ln2_g": 1.0 + nrm(ks[15], (DEPTH, D_MODEL), 0.01),
        "ln2_b": nrm(ks[16], (DEPTH, D_MODEL), 0.01),
    }


def reference(x, w_in, lb_param, hg_norm_g, w_hg_out, conv_w, conv_b, conv_ln_g, conv_ln_b,
              w_conv_out, w_out, ln1_g, ln1_b, w_ffn_in, w_ffn_out, ln2_g, ln2_b):
    dtype = x.dtype
    bsz, seq = x.shape[0], x.shape[1]
    lb_all = jnp.cumsum(jax.nn.softmax(lb_param.astype(jnp.float32), axis=0), axis=0)
    h = x
    for l in range(DEPTH):
        proj = h @ w_in[l]
        q_in, f_logit, i_in, o_gate, glu_v, glu_g, gate_a, gate_b = jnp.split(proj, IN_SPLITS, axis=-1)

        lb = lb_all[l]
        f = lb + (1.0 - lb) * jax.nn.sigmoid(f_logit.astype(jnp.float32))
        k = (1.0 - f).reshape(bsz, seq, HG_HEADS, HG_DK)
        g = jnp.log(f).reshape(bsz, seq, HG_HEADS, HG_DK)
        q = (jax.nn.silu(q_in.astype(jnp.float32)) * HG_DK ** -0.5).reshape(bsz, seq, HG_HEADS, HG_DK)
        v = i_in.astype(jnp.float32).reshape(bsz, seq, HG_HEADS, HG_DV)
        o = hgrn2_recurrence(q, k, v, g)
        o = rms_norm(o, hg_norm_g[l]).reshape(bsz, seq, HG_VWIDTH)
        o = (o * jax.nn.silu(o_gate.astype(jnp.float32))).astype(dtype)
        y_a = o @ w_hg_out[l]

        u = glu_v * jax.nn.sigmoid(glu_g)
        u = causal_depthwise_conv(u, conv_w[l], conv_b[l])
        u = jax.nn.silu(layer_norm(u, conv_ln_g[l], conv_ln_b[l]))
        y_b = u @ w_conv_out[l]

        mixed = jax.nn.sigmoid(gate_a) * y_a + jax.nn.sigmoid(gate_b) * y_b
        h = layer_norm(DEEPNORM_ALPHA * h + mixed @ w_out[l], ln1_g[l], ln1_b[l])

        gu = h @ w_ffn_in[l]
        ffn_g, ffn_u = jnp.split(gu, [FFN_HIDDEN], axis=-1)
        ffn = (jax.nn.silu(ffn_g) * ffn_u) @ w_ffn_out[l]
        h = layer_norm(DEEPNORM_ALPHA * h + ffn, ln2_g[l], ln2_b[l])
    return h
```

```python
import functools

import numpy as np
import jax
import jax.numpy as jnp
from jax import lax
from jax.experimental import pallas as pl
from jax.experimental.pallas import tpu as pltpu

HG_DK = 128
HG_DV = 128
CONV_TAPS = 31
LN_EPS = 1e-5
CONV_HALO = 32
CHUNK = 64
N_LEVELS = CHUNK.bit_length() - 1
MIXER_ROWS = 256
FFN_ROWS = 512
FFN_COLS = 256
V7X_VMEM_BYTES = 64 * 1024 * 1024

BF16 = jnp.bfloat16
F32 = jnp.float32


def _decay_tables():
    c = CHUNK
    sums = np.zeros((2 + N_LEVELS, c, c), np.float32)
    masks = np.zeros((1 + N_LEVELS, c, c), np.float32)
    masks[0] = np.eye(c)
    for r in range(c):
        sums[0, r, :r + 1] = 1.0
        sums[1, r, r + 1:] = 1.0
    for l in range(N_LEVELS):
        m = c >> (l + 1)
        for r in range(c):
            a = (r // (2 * m)) * 2 * m
            if r >= a + m:
                sums[2 + l, r, a + m:r + 1] = 1.0
                masks[1 + l, r, a:a + m] = 1.0
            else:
                sums[2 + l, r, r + 1:a + m] = 1.0
    return sums.reshape((2 + N_LEVELS) * c, c), masks


def _sigmoid(x):
    return 1.0 / (1.0 + jnp.exp(-x))


def _silu(x):
    return x * _sigmoid(x)


def _layer_norm(x, g, b):
    mu = jnp.mean(x, axis=-1, keepdims=True)
    xc = x - mu
    var = jnp.mean(xc * xc, axis=-1, keepdims=True)
    return xc * lax.rsqrt(var + LN_EPS) * g + b


def _dot(a, b):
    return jnp.dot(a, b, preferred_element_type=F32)


def _dot_nt(a, b):
    return lax.dot_general(a, b, (((1,), (1,)), ((), ())), preferred_element_type=F32)


def _dot_tn(a, b):
    return lax.dot_general(a, b, (((0,), (0,)), ((), ())), preferred_element_type=F32)


def _mixer_kernel(layer, alpha, d_model, heads,
                  x_ref, w_in_ref, lbp_ref, sums_ref, masks_ref, hgn_ref, w_hg_ref,
                  cw_ref, cb_ref, clg_ref, clb_ref, w_cv_ref, w_out_ref, l1g_ref, l1b_ref,
                  out_ref,
                  st_s, ubuf_s, q_s, k_s, ghi_s, glo_s, v_s, og_s, ga_s, gb_s, e_s, o_s, ob_s, cv_s):
    rows = x_ref.shape[0]
    width = heads * HG_DK

    @pl.when(pl.program_id(1) == 0)
    def _():
        st_s[...] = jnp.zeros_like(st_s)
        ubuf_s[0:CONV_HALO, :] = jnp.zeros((CONV_HALO, d_model), F32)

    lbp = lbp_ref[...]
    lbe = jnp.exp(lbp - jnp.max(lbp, axis=0, keepdims=True))
    lb = jnp.sum(lbe[0:layer + 1, :], axis=0, keepdims=True) / jnp.sum(lbe, axis=0, keepdims=True)

    xb = x_ref[...].astype(BF16)

    def seg(i):
        return _dot(xb, w_in_ref[:, i * width:(i + 1) * width])

    q_s[...] = _silu(seg(0)) * (HG_DK ** -0.5)
    f = lb + (1.0 - lb) * _sigmoid(seg(1))
    k_s[...] = 1.0 - f
    g = jnp.log(f)
    ghi = g.astype(BF16)
    ghi_s[...] = ghi
    glo_s[...] = (g - ghi.astype(F32)).astype(BF16)
    v_s[...] = seg(2).astype(BF16)
    og_s[...] = _silu(seg(3))
    ubuf_s[CONV_HALO:CONV_HALO + rows, :] = seg(4) * _sigmoid(seg(5))
    ga_s[...] = _sigmoid(seg(6))
    gb_s[...] = _sigmoid(seg(7))

    def chunk_body(c, carry):
        r0 = pl.multiple_of(c * CHUNK, CHUNK)
        rs = pl.ds(r0, CHUNK)
        sums = sums_ref[...]
        e_s[...] = jnp.exp(_dot(sums, ghi_s[rs, :]) + _dot(sums, glo_s[rs, :]))
        for h in range(heads):
            ls = slice(h * HG_DK, (h + 1) * HG_DK)
            q = q_s[rs, ls]
            k = k_s[rs, ls]
            v = v_s[rs, ls]
            st = st_s[h]
            e_b = e_s[0:CHUNK, ls]
            o = _dot_nt((q * e_b).astype(BF16), st.astype(BF16))
            a = masks_ref[0] * _dot_nt(q.astype(BF16), k.astype(BF16))
            for l in range(N_LEVELS):
                e_l = e_s[(2 + l) * CHUNK:(3 + l) * CHUNK, ls]
                a = a + masks_ref[1 + l] * _dot_nt((q * e_l).astype(BF16), (k * e_l).astype(BF16))
            o = o + _dot(a.astype(BF16), v)
            o_s[rs, ls] = o
            k_end = (k * e_s[CHUNK:2 * CHUNK, ls]).astype(BF16)
            st_s[h] = e_b[CHUNK - 1:CHUNK, :] * st + _dot_tn(v, k_end)
        return carry

    lax.fori_loop(0, rows // CHUNK, chunk_body, 0)

    hgn = hgn_ref[...]
    for h in range(heads):
        ls = slice(h * HG_DV, (h + 1) * HG_DV)
        o = o_s[:, ls]
        o = o * lax.rsqrt(jnp.mean(o * o, axis=-1, keepdims=True) + LN_EPS) * hgn
        ob_s[:, ls] = (o * og_s[:, ls]).astype(BF16)
    y_a = _dot(ob_s[...], w_hg_ref[...])

    conv_rows, conv_cols = 64, 256
    for ci in range(d_model // conv_cols):
        cs = slice(ci * conv_cols, (ci + 1) * conv_cols)
        for ri in range(rows // conv_rows):
            base = ri * conv_rows + CONV_HALO - (CONV_TAPS - 1)
            acc = jnp.zeros((conv_rows, conv_cols), F32)
            for j in range(CONV_TAPS):
                acc = acc + ubuf_s[base + j:base + j + conv_rows, cs] * cw_ref[j:j + 1, cs]
            cv_s[ri * conv_rows:(ri + 1) * conv_rows, cs] = acc + cb_ref[:, cs]
    ubuf_s[0:CONV_HALO, :] = ubuf_s[rows:rows + CONV_HALO, :]
    u = _silu(_layer_norm(cv_s[...], clg_ref[...], clb_ref[...]))
    y_b = _dot(u.astype(BF16), w_cv_ref[...])

    mixed = ga_s[...] * y_a + gb_s[...] * y_b
    y = _dot(mixed.astype(BF16), w_out_ref[...])
    out_ref[...] = _layer_norm(alpha * x_ref[...] + y, l1g_ref[...], l1b_ref[...])


def _ffn_kernel(alpha, hidden, h_ref, w_in_ref, w_out_ref, g_ref, b_ref, out_ref, act_s):
    hb = h_ref[...].astype(BF16)
    for c in range(hidden // FFN_COLS):
        gate = _dot(hb, w_in_ref[:, c * FFN_COLS:(c + 1) * FFN_COLS])
        up = _dot(hb, w_in_ref[:, hidden + c * FFN_COLS:hidden + (c + 1) * FFN_COLS])
        act_s[:, c * FFN_COLS:(c + 1) * FFN_COLS] = (_silu(gate) * up).astype(BF16)
    y = _dot(act_s[...], w_out_ref[...])
    out_ref[...] = _layer_norm(alpha * h_ref[...] + y, g_ref[...], b_ref[...])


def _resident(shape):
    return pl.BlockSpec(shape, lambda *_: (0,) * len(shape), pipeline_mode=pl.Buffered(1))


def _mixer_call(layer, alpha, x, w_in, lb_param, hgn, w_hg, cw, cb, clg, clb, w_cv, w_out, l1g, l1b):
    bsz, seq, d_model = x.shape
    heads = d_model // HG_DK
    rows = min(MIXER_ROWS, seq)
    assert seq % rows == 0 and rows % CHUNK == 0 and d_model % 256 == 0 and rows >= CONV_HALO
    sums, masks = _decay_tables()
    sums = jnp.asarray(sums, BF16)
    masks = jnp.asarray(masks, F32)
    operands = (w_in, lb_param, sums, masks, hgn, w_hg, cw, cb, clg, clb, w_cv, w_out, l1g, l1b)
    row_block = pl.BlockSpec((None, rows, d_model), lambda b, s: (b, s, 0))
    act = lambda dt: pltpu.VMEM((rows, d_model), dt)
    return pl.pallas_call(
        functools.partial(_mixer_kernel, layer, alpha, d_model, heads),
        grid=(bsz, seq // rows),
        in_specs=[row_block] + [_resident(a.shape) for a in operands],
        out_specs=row_block,
        out_shape=jax.ShapeDtypeStruct(x.shape, F32),
        scratch_shapes=[
            pltpu.VMEM((heads, HG_DV, HG_DK), F32),
            pltpu.VMEM((rows + CONV_HALO, d_model), F32),
            act(F32), act(F32), act(BF16), act(BF16), act(BF16),
            act(F32), act(F32), act(F32),
            pltpu.VMEM(((2 + N_LEVELS) * CHUNK, d_model), F32),
            act(F32), act(BF16), act(F32),
        ],
        compiler_params=pltpu.CompilerParams(
            dimension_semantics=("arbitrary", "arbitrary"),
            vmem_limit_bytes=V7X_VMEM_BYTES * 7 // 8),
        name="mixer",
    )(x, *operands)


def _ffn_call(alpha, h, w_in, w_out, g, b):
    n, d_model = h.shape
    hidden = w_out.shape[0]
    rows = min(FFN_ROWS, n)
    assert n % rows == 0 and hidden % FFN_COLS == 0
    row_block = pl.BlockSpec((rows, d_model), lambda i: (i, 0))
    return pl.pallas_call(
        functools.partial(_ffn_kernel, alpha, hidden),
        grid=(n // rows,),
        in_specs=[row_block, _resident(w_in.shape), _resident(w_out.shape), _resident(g.shape), _resident(b.shape)],
        out_specs=row_block,
        out_shape=jax.ShapeDtypeStruct(h.shape, F32),
        scratch_shapes=[pltpu.VMEM((rows, hidden), BF16)],
        compiler_params=pltpu.CompilerParams(
            dimension_semantics=("arbitrary",),
            vmem_limit_bytes=V7X_VMEM_BYTES * 7 // 8),
        name="ffn",
    )(h, w_in, w_out, g, b)


def kernel(x, w_in, lb_param, hg_norm_g, w_hg_out, conv_w, conv_b, conv_ln_g, conv_ln_b, w_conv_out, w_out,
           ln1_g, ln1_b, w_ffn_in, w_ffn_out, ln2_g, ln2_b):
    bsz, seq, d_model = x.shape
    depth = w_in.shape[0]
    alpha = (2 * depth) ** 0.25
    row = lambda a: a.reshape(1, -1).astype(F32)
    h = x
    for l in range(depth):
        h = _mixer_call(l, alpha, h, w_in[l].astype(BF16), lb_param.astype(F32), row(hg_norm_g[l]),
                        w_hg_out[l].astype(BF16), conv_w[l].astype(F32), row(conv_b[l]), row(conv_ln_g[l]),
                        row(conv_ln_b[l]), w_conv_out[l].astype(BF16), w_out[l].astype(BF16),
                        row(ln1_g[l]), row(ln1_b[l]))
        h = _ffn_call(alpha, h.reshape(bsz * seq, d_model), w_ffn_in[l].astype(BF16), w_ffn_out[l].astype(BF16),
                      row(ln2_g[l]), row(ln2_b[l])).reshape(bsz, seq, d_model)
    return h
```

```python
import functools

import numpy as np
import jax
import jax.numpy as jnp
from jax import lax
from jax.experimental import pallas as pl
from jax.experimental.pallas import tpu as pltpu

HG_DK = 128
HG_DV = 128
LANES = 128
SUBLANES = 8
CONV_TAPS = 31
LN_EPS = 1e-5
CONV_HALO = 32
CHUNK = 64
N_LEVELS = CHUNK.bit_length() - 1
N_TABLE_LEVELS = 3
MIXER_ROWS = 256
PROJ_COLS = 256
CONV_ROWS = 64
FFN_ROWS = 512
FFN_COLS = 256
V7X_VMEM_BYTES = 64 * 1024 * 1024

BF16 = jnp.bfloat16
F32 = jnp.float32


def _decay_tables():
    c = CHUNK
    sums = np.zeros((1 + N_TABLE_LEVELS, c, c), np.float32)
    masks = np.zeros((1 + N_LEVELS, c, c), np.float32)
    masks[0] = np.eye(c)
    for r in range(c):
        sums[0, r, :r + 1] = 1.0
    for l in range(N_LEVELS):
        m = c >> (l + 1)
        i = l - (N_LEVELS - N_TABLE_LEVELS)
        for r in range(c):
            a = (r // (2 * m)) * 2 * m
            if r >= a + m:
                masks[1 + l, r, a:a + m] = 1.0
                if i >= 0:
                    sums[1 + i, r, a + m:r + 1] = 1.0
            elif i >= 0:
                sums[1 + i, r, r + 1:a + m] = 1.0
    sums = sums.reshape((1 + N_TABLE_LEVELS) * c, c)
    return np.concatenate([sums, sums], axis=1), masks


def _sigmoid(x):
    return 1.0 / (1.0 + jnp.exp(-x))


def _silu(x):
    return x * _sigmoid(x)


def _layer_norm(x, g, b):
    mu = jnp.mean(x, axis=-1, keepdims=True)
    xc = x - mu
    var = jnp.mean(xc * xc, axis=-1, keepdims=True)
    return xc * lax.rsqrt(var + LN_EPS) * g + b


def _dot(a, b):
    return jnp.dot(a, b, preferred_element_type=F32)


def _dot_nt(a, b):
    return lax.dot_general(a, b, (((1,), (1,)), ((), ())), preferred_element_type=F32)


def _dot_tn(a, b):
    return lax.dot_general(a, b, (((0,), (0,)), ((), ())), preferred_element_type=F32)


def _block_level_exponent(b_ref, ls, m):
    parts = []
    for a in range(0, CHUNK, 2 * m):
        beta = b_ref[a + m - 1:a + m, ls]
        parts.append(beta - b_ref[a:a + m, ls])
        parts.append(b_ref[a + m:a + 2 * m, ls] - beta)
    return jnp.concatenate(parts, axis=0)


def _mixer_kernel(layer, alpha, d_model, heads,
                  x_ref, w_in_ref, lbp_ref, sums_ref, masks_ref, hgn_ref, w_hg_ref,
                  cw_ref, cb_ref, clg_ref, clb_ref, w_cv_ref, w_out_ref, l1g_ref, l1b_ref,
                  out_ref,
                  st_s, ubuf_s, q_s, k_s, g_s, v_s, og_s, ga_s, gb_s, e_s, ob_s, cv_s, ub_s):
    rows = x_ref.shape[0]
    width = heads * HG_DK
    lane_tiles = d_model // LANES

    @pl.when(pl.program_id(1) == 0)
    def _():
        st_s[...] = jnp.zeros_like(st_s)
        ubuf_s[:, 0:CONV_HALO, :] = jnp.zeros((lane_tiles, CONV_HALO, LANES), F32)

    lbp = lbp_ref[...]
    lbe = jnp.exp(lbp - jnp.max(lbp, axis=0, keepdims=True))
    lb = jnp.sum(lbe[0:layer + 1, :], axis=0, keepdims=True) / jnp.sum(lbe, axis=0, keepdims=True)

    xb = x_ref[...].astype(BF16)

    def proj(i, cols):
        return _dot(xb, w_in_ref[:, i * width + cols.start:i * width + cols.stop])

    def glu_piece(cols):
        u = proj(4, cols) * _sigmoid(proj(5, cols))
        for ci in range(cols.start // LANES, cols.stop // LANES):
            ubuf_s[ci, CONV_HALO:CONV_HALO + rows, :] = u[:, ci * LANES - cols.start:(ci + 1) * LANES - cols.start]

    def q_piece(cols):
        q_s[:, cols] = (_silu(proj(0, cols)) * (HG_DK ** -0.5)).astype(BF16)

    def f_piece(cols):
        f = lb[:, cols] + (1.0 - lb[:, cols]) * _sigmoid(proj(1, cols))
        k_s[:, cols] = (1.0 - f).astype(BF16)
        g = jnp.log2(f)
        ghi = g.astype(BF16)
        glo = (g - ghi.astype(F32)).astype(BF16)
        for c in range(rows // CHUNK):
            g_s[c, 0:CHUNK, cols] = ghi[c * CHUNK:(c + 1) * CHUNK, :]
            g_s[c, CHUNK:2 * CHUNK, cols] = glo[c * CHUNK:(c + 1) * CHUNK, :]

    def v_piece(cols):
        v_s[:, cols] = proj(2, cols).astype(BF16)

    def og_piece(cols):
        og_s[:, cols] = _silu(proj(3, cols))

    def ga_piece(cols):
        ga_s[:, cols] = _sigmoid(proj(6, cols))

    def gb_piece(cols):
        gb_s[:, cols] = _sigmoid(proj(7, cols))

    def conv_piece(ci, r0):
        cs = slice(ci * LANES, (ci + 1) * LANES)
        for ph in range(2):
            acc = jnp.zeros((CONV_ROWS // 2, LANES), F32)
            for j in range(CONV_TAPS):
                start = r0 + (CONV_HALO - (CONV_TAPS - 1) + j + ph)
                acc = acc + ubuf_s[ci, pl.ds(start, CONV_ROWS // 2, stride=2), :] * cw_ref[j:j + 1, cs]
            cv_s[ci, pl.ds(r0 + ph, CONV_ROWS // 2, stride=2), :] = acc + cb_ref[:, cs]

    def conv_norm_piece(r0):
        rs = slice(r0, r0 + CONV_ROWS)
        cv = jnp.concatenate([cv_s[ci, rs, :] for ci in range(lane_tiles)], axis=-1)
        ub_s[rs, :] = _silu(_layer_norm(cv, clg_ref[...], clb_ref[...])).astype(BF16)

    col_pieces = [slice(c0, c0 + PROJ_COLS) for c0 in range(0, width, PROJ_COLS)]
    for cols in col_pieces:
        glu_piece(cols)
    vector_work = [functools.partial(conv_piece, ci, r0)
                   for ci in range(lane_tiles) for r0 in range(0, rows, CONV_ROWS)]
    vector_work += [functools.partial(conv_norm_piece, r0) for r0 in range(0, rows, CONV_ROWS)]
    segments = (q_piece, f_piece, v_piece, og_piece, ga_piece, gb_piece)
    for i, fn in enumerate(segments):
        for cols in col_pieces:
            fn(cols)
        for work in vector_work[len(vector_work) * i // len(segments):len(vector_work) * (i + 1) // len(segments)]:
            work()
    for ci in range(lane_tiles):
        ubuf_s[ci, 0:CONV_HALO, :] = ubuf_s[ci, rows:rows + CONV_HALO, :]

    hgn = hgn_ref[...]
    n_block_levels = N_LEVELS - N_TABLE_LEVELS

    def chunk_body(c, carry):
        r0 = pl.multiple_of(c * CHUNK, CHUNK)
        rs = pl.ds(r0, CHUNK)

        e_s[...] = _dot(sums_ref[...], g_s[c])
        o_inter, scores = [], []
        for h in range(heads):
            ls = slice(h * HG_DK, (h + 1) * HG_DK)
            q = q_s[rs, ls]
            k = k_s[rs, ls]
            st = st_s[h]
            e_b = jnp.exp2(e_s[0:CHUNK, ls]).astype(BF16)
            o_inter.append(_dot_nt(q * e_b, st.astype(BF16)))
            a = masks_ref[0] * _dot_nt(q, k)
            for l in range(N_LEVELS):
                if l < n_block_levels:
                    e_l = jnp.exp2(_block_level_exponent(e_s, ls, CHUNK >> (l + 1))).astype(BF16)
                else:
                    i = 1 + l - n_block_levels
                    e_l = jnp.exp2(e_s[i * CHUNK:(i + 1) * CHUNK, ls]).astype(BF16)
                a = a + masks_ref[1 + l] * _dot_nt(q * e_l, k * e_l)
            scores.append(a.astype(BF16))
            b_end = e_s[CHUNK - 1:CHUNK, ls]
            k_end = k * jnp.exp2(b_end - e_s[0:CHUNK, ls]).astype(BF16)
            st_s[h] = jnp.exp2(b_end) * st + _dot_tn(v_s[rs, ls], k_end)
        for h in range(heads):
            ls = slice(h * HG_DK, (h + 1) * HG_DK)
            o = o_inter[h] + _dot(scores[h], v_s[rs, ls])
            o = o * lax.rsqrt(jnp.mean(o * o, axis=-1, keepdims=True) + LN_EPS) * hgn
            ob_s[rs, ls] = (o * og_s[rs, ls]).astype(BF16)

        return carry

    lax.fori_loop(0, rows // CHUNK, chunk_body, 0)

    y_a = _dot(ob_s[...], w_hg_ref[...])
    y_b = _dot(ub_s[...], w_cv_ref[...])
    mixed = ga_s[...] * y_a + gb_s[...] * y_b
    y = _dot(mixed.astype(BF16), w_out_ref[...])
    out_ref[...] = _layer_norm(alpha * x_ref[...] + y, l1g_ref[...], l1b_ref[...])


def _ffn_kernel(alpha, hidden, h_ref, w_in_ref, w_out_ref, g_ref, b_ref, out_ref, act_s):
    hb = h_ref[...].astype(BF16)
    for c in range(hidden // FFN_COLS):
        gate = _dot(hb, w_in_ref[:, c * FFN_COLS:(c + 1) * FFN_COLS])
        up = _dot(hb, w_in_ref[:, hidden + c * FFN_COLS:hidden + (c + 1) * FFN_COLS])
        act_s[:, c * FFN_COLS:(c + 1) * FFN_COLS] = (_silu(gate) * up).astype(BF16)
    y = _dot(act_s[...], w_out_ref[...])
    out_ref[...] = _layer_norm(alpha * h_ref[...] + y, g_ref[...], b_ref[...])


def _resident(shape):
    return pl.BlockSpec(shape, lambda *_: (0,) * len(shape), pipeline_mode=pl.Buffered(1))


def _mixer_call(layer, alpha, x, w_in, lb_param, hgn, w_hg, cw, cb, clg, clb, w_cv, w_out, l1g, l1b):
    bsz, seq, d_model = x.shape
    heads = d_model // HG_DK
    rows = min(MIXER_ROWS, seq)
    assert seq % rows == 0 and rows % CHUNK == 0 and d_model % LANES == 0 and rows >= CONV_HALO
    assert N_TABLE_LEVELS <= N_LEVELS and (CHUNK >> (N_LEVELS - N_TABLE_LEVELS)) % SUBLANES == 0
    sums, masks = _decay_tables()
    sums = jnp.asarray(sums, BF16)
    masks = jnp.asarray(masks, F32)
    operands = (w_in, lb_param, sums, masks, hgn, w_hg, cw, cb, clg, clb, w_cv, w_out, l1g, l1b)
    row_block = pl.BlockSpec((None, rows, d_model), lambda b, s: (b, s, 0))
    act = lambda dt: pltpu.VMEM((rows, d_model), dt)
    return pl.pallas_call(
        functools.partial(_mixer_kernel, layer, alpha, d_model, heads),
        grid=(bsz, seq // rows),
        in_specs=[row_block] + [_resident(a.shape) for a in operands],
        out_specs=row_block,
        out_shape=jax.ShapeDtypeStruct(x.shape, F32),
        scratch_shapes=[
            pltpu.VMEM((heads, HG_DV, HG_DK), F32),
            pltpu.VMEM((d_model // LANES, rows + CONV_HALO, LANES), F32),
            act(BF16), act(BF16),
            pltpu.VMEM((rows // CHUNK, 2 * CHUNK, d_model), BF16),
            act(BF16),
            act(F32), act(F32), act(F32),
            pltpu.VMEM(((1 + N_TABLE_LEVELS) * CHUNK, d_model), F32),
            act(BF16),
            pltpu.VMEM((d_model // LANES, rows, LANES), F32),
            act(BF16),
        ],
        compiler_params=pltpu.CompilerParams(
            dimension_semantics=("arbitrary", "arbitrary"),
            vmem_limit_bytes=V7X_VMEM_BYTES * 7 // 8),
        name="mixer",
    )(x, *operands)


def _ffn_call(alpha, h, w_in, w_out, g, b):
    n, d_model = h.shape
    hidden = w_out.shape[0]
    rows = min(FFN_ROWS, n)
    assert n % rows == 0 and hidden % FFN_COLS == 0
    row_block = pl.BlockSpec((rows, d_model), lambda i: (i, 0))
    return pl.pallas_call(
        functools.partial(_ffn_kernel, alpha, hidden),
        grid=(n // rows,),
        in_specs=[row_block, _resident(w_in.shape), _resident(w_out.shape), _resident(g.shape), _resident(b.shape)],
        out_specs=row_block,
        out_shape=jax.ShapeDtypeStruct(h.shape, F32),
        scratch_shapes=[pltpu.VMEM((rows, hidden), BF16)],
        compiler_params=pltpu.CompilerParams(
            dimension_semantics=("arbitrary",),
            vmem_limit_bytes=V7X_VMEM_BYTES * 7 // 8),
        name="ffn",
    )(h, w_in, w_out, g, b)


def kernel(x, w_in, lb_param, hg_norm_g, w_hg_out, conv_w, conv_b, conv_ln_g, conv_ln_b, w_conv_out, w_out,
           ln1_g, ln1_b, w_ffn_in, w_ffn_out, ln2_g, ln2_b):
    bsz, seq, d_model = x.shape
    depth = w_in.shape[0]
    alpha = (2 * depth) ** 0.25
    row = lambda a: a.reshape(1, -1).astype(F32)
    h = x
    for l in range(depth):
        h = _mixer_call(l, alpha, h, w_in[l].astype(BF16), lb_param.astype(F32), row(hg_norm_g[l]),
                        w_hg_out[l].astype(BF16), conv_w[l].astype(F32), row(conv_b[l]), row(conv_ln_g[l]),
                        row(conv_ln_b[l]), w_conv_out[l].astype(BF16), w_out[l].astype(BF16),
                        row(ln1_g[l]), row(ln1_b[l]))
        h = _ffn_call(alpha, h.reshape(bsz * seq, d_model), w_ffn_in[l].astype(BF16), w_ffn_out[l].astype(BF16),
                      row(ln2_g[l]), row(ln2_b[l])).reshape(bsz, seq, d_model)
    return h
```

```python
import functools

import numpy as np
import jax
import jax.numpy as jnp
from jax import lax
from jax.experimental import pallas as pl
from jax.experimental.pallas import tpu as pltpu

HG_DK = 128
HG_DV = 128
LANES = 128
SUBLANES = 8
CONV_TAPS = 31
LN_EPS = 1e-5
CONV_HALO = 32
CHUNK = 64
N_LEVELS = CHUNK.bit_length() - 1
N_TABLE_LEVELS = 3
INTRA_CHUNKS = 4
MIXER_ROWS = 256
PROJ_COLS = 256
CONV_ROWS = 64
FFN_ROWS = 512
FFN_COLS = 256
V7X_VMEM_BYTES = 64 * 1024 * 1024

BF16 = jnp.bfloat16
F32 = jnp.float32


def _decay_tables():
    c = CHUNK
    sums = np.zeros((1 + N_TABLE_LEVELS, c, c), np.float32)
    masks = np.zeros((1 + N_LEVELS, c, c), np.float32)
    masks[0] = np.eye(c)
    for r in range(c):
        sums[0, r, :r + 1] = 1.0
    for l in range(N_LEVELS):
        m = c >> (l + 1)
        i = l - (N_LEVELS - N_TABLE_LEVELS)
        for r in range(c):
            a = (r // (2 * m)) * 2 * m
            if r >= a + m:
                masks[1 + l, r, a:a + m] = 1.0
                if i >= 0:
                    sums[1 + i, r, a + m:r + 1] = 1.0
            elif i >= 0:
                sums[1 + i, r, r + 1:a + m] = 1.0
    sums = sums.reshape((1 + N_TABLE_LEVELS) * c, c)
    return np.concatenate([sums, sums], axis=1), masks


def _sigmoid(x):
    return 1.0 / (1.0 + jnp.exp(-x))


def _silu(x):
    return x * _sigmoid(x)


def _layer_norm(x, g, b):
    mu = jnp.mean(x, axis=-1, keepdims=True)
    xc = x - mu
    var = jnp.mean(xc * xc, axis=-1, keepdims=True)
    return xc * lax.rsqrt(var + LN_EPS) * g + b


def _dot(a, b):
    return jnp.dot(a, b, preferred_element_type=F32)


def _dot_nt(a, b):
    return lax.dot_general(a, b, (((1,), (1,)), ((), ())), preferred_element_type=F32)


def _dot_tn(a, b):
    return lax.dot_general(a, b, (((0,), (0,)), ((), ())), preferred_element_type=F32)


def _block_level_exponent(b_ref, ls, m):
    parts = []
    for a in range(0, CHUNK, 2 * m):
        beta = b_ref[a + m - 1:a + m, ls]
        parts.append(beta - b_ref[a:a + m, ls])
        parts.append(b_ref[a + m:a + 2 * m, ls] - beta)
    return jnp.concatenate(parts, axis=0)


def _mixer_kernel(layer, alpha, d_model, heads,
                  x_ref, w_in_ref, lbp_ref, sums_ref, masks_ref, hgn_ref, w_hg_ref,
                  cw_ref, cb_ref, clg_ref, clb_ref, w_cv_ref, w_out_ref, l1g_ref, l1b_ref,
                  out_ref,
                  st_s, ubuf_s, q_s, k_s, g_s, v_s, og_s, ga_s, gb_s, e_s, qd_s, a_s, ke_s, dec_s, ob_s, cv_s, ub_s):
    rows = x_ref.shape[0]
    width = heads * HG_DK
    lane_tiles = d_model // LANES

    @pl.when(pl.program_id(1) == 0)
    def _():
        st_s[...] = jnp.zeros_like(st_s)
        ubuf_s[:, 0:CONV_HALO, :] = jnp.zeros((lane_tiles, CONV_HALO, LANES), F32)

    lbp = lbp_ref[...]
    lbe = jnp.exp(lbp - jnp.max(lbp, axis=0, keepdims=True))
    lb = jnp.sum(lbe[0:layer + 1, :], axis=0, keepdims=True) / jnp.sum(lbe, axis=0, keepdims=True)

    xb = x_ref[...].astype(BF16)

    def proj(i, cols):
        return _dot(xb, w_in_ref[:, i * width + cols.start:i * width + cols.stop])

    def glu_piece(cols):
        u = proj(4, cols) * _sigmoid(proj(5, cols))
        for ci in range(cols.start // LANES, cols.stop // LANES):
            ubuf_s[ci, CONV_HALO:CONV_HALO + rows, :] = u[:, ci * LANES - cols.start:(ci + 1) * LANES - cols.start]

    def q_piece(cols):
        q_s[:, cols] = (_silu(proj(0, cols)) * (HG_DK ** -0.5)).astype(BF16)

    def f_piece(cols):
        f = lb[:, cols] + (1.0 - lb[:, cols]) * _sigmoid(proj(1, cols))
        k_s[:, cols] = (1.0 - f).astype(BF16)
        g = jnp.log2(f)
        ghi = g.astype(BF16)
        glo = (g - ghi.astype(F32)).astype(BF16)
        for c in range(rows // CHUNK):
            g_s[c, 0:CHUNK, cols] = ghi[c * CHUNK:(c + 1) * CHUNK, :]
            g_s[c, CHUNK:2 * CHUNK, cols] = glo[c * CHUNK:(c + 1) * CHUNK, :]

    def v_piece(cols):
        v_s[:, cols] = proj(2, cols).astype(BF16)

    def og_piece(cols):
        og_s[:, cols] = _silu(proj(3, cols))

    def ga_piece(cols):
        ga_s[:, cols] = _sigmoid(proj(6, cols))

    def gb_piece(cols):
        gb_s[:, cols] = _sigmoid(proj(7, cols))

    def conv_piece(ci, r0):
        cs = slice(ci * LANES, (ci + 1) * LANES)
        for ph in range(2):
            acc = jnp.zeros((CONV_ROWS // 2, LANES), F32)
            for j in range(CONV_TAPS):
                start = r0 + (CONV_HALO - (CONV_TAPS - 1) + j + ph)
                acc = acc + ubuf_s[ci, pl.ds(start, CONV_ROWS // 2, stride=2), :] * cw_ref[j:j + 1, cs]
            cv_s[ci, pl.ds(r0 + ph, CONV_ROWS // 2, stride=2), :] = acc + cb_ref[:, cs]

    def conv_norm_piece(r0):
        rs = slice(r0, r0 + CONV_ROWS)
        cv = jnp.concatenate([cv_s[ci, rs, :] for ci in range(lane_tiles)], axis=-1)
        ub_s[rs, :] = _silu(_layer_norm(cv, clg_ref[...], clb_ref[...])).astype(BF16)

    col_pieces = [slice(c0, c0 + PROJ_COLS) for c0 in range(0, width, PROJ_COLS)]
    for cols in col_pieces:
        glu_piece(cols)
    vector_work = [functools.partial(conv_piece, ci, r0)
                   for ci in range(lane_tiles) for r0 in range(0, rows, CONV_ROWS)]
    vector_work += [functools.partial(conv_norm_piece, r0) for r0 in range(0, rows, CONV_ROWS)]
    segments = (q_piece, f_piece, v_piece, og_piece, ga_piece, gb_piece)
    for i, fn in enumerate(segments):
        for cols in col_pieces:
            fn(cols)
        for work in vector_work[len(vector_work) * i // len(segments):len(vector_work) * (i + 1) // len(segments)]:
            work()
    for ci in range(lane_tiles):
        ubuf_s[ci, 0:CONV_HALO, :] = ubuf_s[ci, rows:rows + CONV_HALO, :]

    hgn = hgn_ref[...]
    n_block_levels = N_LEVELS - N_TABLE_LEVELS

    def intra_body(it, carry):
        for sub in range(INTRA_CHUNKS):
            c = it * INTRA_CHUNKS + sub
            rs = pl.ds(pl.multiple_of(c * CHUNK, CHUNK), CHUNK)
            e_c = e_s.at[sub]
            e_c[...] = _dot(sums_ref[...], g_s[c])
            for h in range(heads):
                ls = slice(h * HG_DK, (h + 1) * HG_DK)
                q = q_s[rs, ls]
                k = k_s[rs, ls]
                qd_s[rs, ls] = q * jnp.exp2(e_c[0:CHUNK, ls]).astype(BF16)
                a = masks_ref[0] * _dot_nt(q, k)
                for l in range(N_LEVELS):
                    if l < n_block_levels:
                        e_l = jnp.exp2(_block_level_exponent(e_c, ls, CHUNK >> (l + 1))).astype(BF16)
                    else:
                        i = 1 + l - n_block_levels
                        e_l = jnp.exp2(e_c[i * CHUNK:(i + 1) * CHUNK, ls]).astype(BF16)
                    a = a + masks_ref[1 + l] * _dot_nt(q * e_l, k * e_l)
                a_s[c, h] = a.astype(BF16)
                b_end = e_c[CHUNK - 1:CHUNK, ls]
                ke_s[rs, ls] = k * jnp.exp2(b_end - e_c[0:CHUNK, ls]).astype(BF16)
                dec_s[c, 0:1, ls] = jnp.exp2(b_end)
        return carry

    lax.fori_loop(0, rows // (CHUNK * INTRA_CHUNKS), intra_body, 0)

    for c in range(rows // CHUNK):
        rs = slice(c * CHUNK, (c + 1) * CHUNK)
        outs = []
        for h in range(heads):
            ls = slice(h * HG_DK, (h + 1) * HG_DK)
            st = st_s[h]
            v = v_s[rs, ls]
            outs.append(_dot_nt(qd_s[rs, ls], st.astype(BF16)) + _dot(a_s[c, h], v))
            st_s[h] = dec_s[c, 0:1, ls] * st + _dot_tn(v, ke_s[rs, ls])
        for h in range(heads):
            ls = slice(h * HG_DK, (h + 1) * HG_DK)
            o = outs[h]
            o = o * lax.rsqrt(jnp.mean(o * o, axis=-1, keepdims=True) + LN_EPS) * hgn
            ob_s[rs, ls] = (o * og_s[rs, ls]).astype(BF16)

    y_a = _dot(ob_s[...], w_hg_ref[...])
    y_b = _dot(ub_s[...], w_cv_ref[...])
    mixed = ga_s[...] * y_a + gb_s[...] * y_b
    y = _dot(mixed.astype(BF16), w_out_ref[...])
    out_ref[...] = _layer_norm(alpha * x_ref[...] + y, l1g_ref[...], l1b_ref[...])


def _ffn_kernel(alpha, hidden, h_ref, w_in_ref, w_out_ref, g_ref, b_ref, out_ref, act_s):
    hb = h_ref[...].astype(BF16)
    for c in range(hidden // FFN_COLS):
        gate = _dot(hb, w_in_ref[:, c * FFN_COLS:(c + 1) * FFN_COLS])
        up = _dot(hb, w_in_ref[:, hidden + c * FFN_COLS:hidden + (c + 1) * FFN_COLS])
        act_s[:, c * FFN_COLS:(c + 1) * FFN_COLS] = (_silu(gate) * up).astype(BF16)
    y = _dot(act_s[...], w_out_ref[...])
    out_ref[...] = _layer_norm(alpha * h_ref[...] + y, g_ref[...], b_ref[...])


def _resident(shape):
    return pl.BlockSpec(shape, lambda *_: (0,) * len(shape), pipeline_mode=pl.Buffered(1))


def _mixer_call(layer, alpha, x, w_in, lb_param, hgn, w_hg, cw, cb, clg, clb, w_cv, w_out, l1g, l1b):
    bsz, seq, d_model = x.shape
    heads = d_model // HG_DK
    rows = min(MIXER_ROWS, seq)
    assert seq % rows == 0 and rows % CHUNK == 0 and d_model % LANES == 0 and rows >= CONV_HALO
    assert N_TABLE_LEVELS <= N_LEVELS and (CHUNK >> (N_LEVELS - N_TABLE_LEVELS)) % SUBLANES == 0
    sums, masks = _decay_tables()
    sums = jnp.asarray(sums, BF16)
    masks = jnp.asarray(masks, F32)
    operands = (w_in, lb_param, sums, masks, hgn, w_hg, cw, cb, clg, clb, w_cv, w_out, l1g, l1b)
    row_block = pl.BlockSpec((None, rows, d_model), lambda b, s: (b, s, 0))
    act = lambda dt: pltpu.VMEM((rows, d_model), dt)
    return pl.pallas_call(
        functools.partial(_mixer_kernel, layer, alpha, d_model, heads),
        grid=(bsz, seq // rows),
        in_specs=[row_block] + [_resident(a.shape) for a in operands],
        out_specs=row_block,
        out_shape=jax.ShapeDtypeStruct(x.shape, F32),
        scratch_shapes=[
            pltpu.VMEM((heads, HG_DV, HG_DK), F32),
            pltpu.VMEM((d_model // LANES, rows + CONV_HALO, LANES), F32),
            act(BF16), act(BF16),
            pltpu.VMEM((rows // CHUNK, 2 * CHUNK, d_model), BF16),
            act(BF16),
            act(F32), act(F32), act(F32),
            pltpu.VMEM((INTRA_CHUNKS, (1 + N_TABLE_LEVELS) * CHUNK, d_model), F32),
            act(BF16),
            pltpu.VMEM((rows // CHUNK, heads, CHUNK, CHUNK), BF16),
            act(BF16),
            pltpu.VMEM((rows // CHUNK, SUBLANES, d_model), F32),
            act(BF16),
            pltpu.VMEM((d_model // LANES, rows, LANES), F32),
            act(BF16),
        ],
        compiler_params=pltpu.CompilerParams(
            dimension_semantics=("arbitrary", "arbitrary"),
            vmem_limit_bytes=V7X_VMEM_BYTES * 7 // 8),
        name="mixer",
    )(x, *operands)


def _ffn_call(alpha, h, w_in, w_out, g, b):
    n, d_model = h.shape
    hidden = w_out.shape[0]
    rows = min(FFN_ROWS, n)
    assert n % rows == 0 and hidden % FFN_COLS == 0
    row_block = pl.BlockSpec((rows, d_model), lambda i: (i, 0))
    return pl.pallas_call(
        functools.partial(_ffn_kernel, alpha, hidden),
        grid=(n // rows,),
        in_specs=[row_block, _resident(w_in.shape), _resident(w_out.shape), _resident(g.shape), _resident(b.shape)],
        out_specs=row_block,
        out_shape=jax.ShapeDtypeStruct(h.shape, F32),
        scratch_shapes=[pltpu.VMEM((rows, hidden), BF16)],
        compiler_params=pltpu.CompilerParams(
            dimension_semantics=("arbitrary",),
            vmem_limit_bytes=V7X_VMEM_BYTES * 7 // 8),
        name="ffn",
    )(h, w_in, w_out, g, b)


def kernel(x, w_in, lb_param, hg_norm_g, w_hg_out, conv_w, conv_b, conv_ln_g, conv_ln_b, w_conv_out, w_out,
           ln1_g, ln1_b, w_ffn_in, w_ffn_out, ln2_g, ln2_b):
    bsz, seq, d_model = x.shape
    depth = w_in.shape[0]
    alpha = (2 * depth) ** 0.25
    row = lambda a: a.reshape(1, -1).astype(F32)
    h = x
    for l in range(depth):
        h = _mixer_call(l, alpha, h, w_in[l].astype(BF16), lb_param.astype(F32), row(hg_norm_g[l]),
                        w_hg_out[l].astype(BF16), conv_w[l].astype(F32), row(conv_b[l]), row(conv_ln_g[l]),
                        row(conv_ln_b[l]), w_conv_out[l].astype(BF16), w_out[l].astype(BF16),
                        row(ln1_g[l]), row(ln1_b[l]))
        h = _ffn_call(alpha, h.reshape(bsz * seq, d_model), w_ffn_in[l].astype(BF16), w_ffn_out[l].astype(BF16),
                      row(ln2_g[l]), row(ln2_b[l])).reshape(bsz, seq, d_model)
    return h
```

```python
import functools

import numpy as np
import jax
import jax.numpy as jnp
from jax import lax
from jax.experimental import pallas as pl
from jax.experimental.pallas import tpu as pltpu

HG_DK = 128
HG_DV = 128
LANES = 128
SUBLANES = 8
CONV_TAPS = 31
LN_EPS = 1e-5
CONV_HALO = 32
CHUNK = 64
N_LEVELS = CHUNK.bit_length() - 1
N_TABLE_LEVELS = 3
INTRA_CHUNKS = 4
MIXER_ROWS = 256
PROJ_COLS = 256
CONV_ROWS = 64
FFN_ROWS = 1024
FFN_COLS = 256
V7X_VMEM_BYTES = 64 * 1024 * 1024

BF16 = jnp.bfloat16
F32 = jnp.float32


def _decay_tables():
    c = CHUNK
    sums = np.zeros((1 + N_TABLE_LEVELS, c, c), np.float32)
    masks = np.zeros((1 + N_LEVELS, c, c), np.float32)
    masks[0] = np.eye(c)
    for r in range(c):
        sums[0, r, :r + 1] = 1.0
    for l in range(N_LEVELS):
        m = c >> (l + 1)
        i = l - (N_LEVELS - N_TABLE_LEVELS)
        for r in range(c):
            a = (r // (2 * m)) * 2 * m
            if r >= a + m:
                masks[1 + l, r, a:a + m] = 1.0
                if i >= 0:
                    sums[1 + i, r, a + m:r + 1] = 1.0
            elif i >= 0:
                sums[1 + i, r, r + 1:a + m] = 1.0
    sums = sums.reshape((1 + N_TABLE_LEVELS) * c, c)
    return np.concatenate([sums, sums], axis=1), masks


def _sigmoid(x):
    return 1.0 / (1.0 + jnp.exp(-x))


def _silu(x):
    return x * _sigmoid(x)


def _layer_norm(x, g, b):
    mu = jnp.mean(x, axis=-1, keepdims=True)
    xc = x - mu
    var = jnp.mean(xc * xc, axis=-1, keepdims=True)
    return xc * lax.rsqrt(var + LN_EPS) * g + b


def _dot(a, b):
    return jnp.dot(a, b, preferred_element_type=F32)


def _dot_nt(a, b):
    return lax.dot_general(a, b, (((1,), (1,)), ((), ())), preferred_element_type=F32)


def _dot_tn(a, b):
    return lax.dot_general(a, b, (((0,), (0,)), ((), ())), preferred_element_type=F32)


def _block_level_exponent(b_ref, ls, m):
    parts = []
    for a in range(0, CHUNK, 2 * m):
        beta = b_ref[a + m - 1:a + m, ls]
        parts.append(beta - b_ref[a:a + m, ls])
        parts.append(b_ref[a + m:a + 2 * m, ls] - beta)
    return jnp.concatenate(parts, axis=0)


def _mixer_kernel(layer, alpha, d_model, heads,
                  x_ref, w_in_ref, lbp_ref, sums_ref, masks_ref, hgn_ref, w_hg_ref,
                  cw_ref, cb_ref, clg_ref, clb_ref, w_cv_ref, w_out_ref, l1g_ref, l1b_ref,
                  out_ref,
                  st_s, ubuf_s, q_s, k_s, g_s, v_s, og_s, ga_s, gb_s, e_s, qd_s, a_s, ke_s, dec_s, ob_s, cv_s, ub_s):
    rows = x_ref.shape[0]
    width = heads * HG_DK
    lane_tiles = d_model // LANES

    @pl.when(pl.program_id(1) == 0)
    def _():
        st_s[...] = jnp.zeros_like(st_s)
        ubuf_s[:, 0:CONV_HALO, :] = jnp.zeros((lane_tiles, CONV_HALO, LANES), F32)

    lbp = lbp_ref[...]
    lbe = jnp.exp(lbp - jnp.max(lbp, axis=0, keepdims=True))
    lb = jnp.sum(lbe[0:layer + 1, :], axis=0, keepdims=True) / jnp.sum(lbe, axis=0, keepdims=True)

    xb = x_ref[...].astype(BF16)

    def proj(i, cols):
        return _dot(xb, w_in_ref[:, i * width + cols.start:i * width + cols.stop])

    def glu_piece(cols):
        u = proj(4, cols) * _sigmoid(proj(5, cols))
        for ci in range(cols.start // LANES, cols.stop // LANES):
            ubuf_s[ci, CONV_HALO:CONV_HALO + rows, :] = u[:, ci * LANES - cols.start:(ci + 1) * LANES - cols.start]

    def q_piece(cols):
        q_s[:, cols] = (_silu(proj(0, cols)) * (HG_DK ** -0.5)).astype(BF16)

    def f_piece(cols):
        f = lb[:, cols] + (1.0 - lb[:, cols]) * _sigmoid(proj(1, cols))
        k_s[:, cols] = (1.0 - f).astype(BF16)
        g = jnp.log2(f)
        ghi = g.astype(BF16)
        glo = (g - ghi.astype(F32)).astype(BF16)
        for c in range(rows // CHUNK):
            g_s[c, 0:CHUNK, cols] = ghi[c * CHUNK:(c + 1) * CHUNK, :]
            g_s[c, CHUNK:2 * CHUNK, cols] = glo[c * CHUNK:(c + 1) * CHUNK, :]

    def v_piece(cols):
        v_s[:, cols] = proj(2, cols).astype(BF16)

    def og_piece(cols):
        gain = jnp.concatenate([hgn_ref[...]] * (PROJ_COLS // HG_DV), axis=-1)
        og_s[:, cols] = _silu(proj(3, cols)) * gain

    def ga_piece(cols):
        ga_s[:, cols] = _sigmoid(proj(6, cols))

    def gb_piece(cols):
        gb_s[:, cols] = _sigmoid(proj(7, cols))

    def conv_piece(ci, r0):
        cs = slice(ci * LANES, (ci + 1) * LANES)
        for ph in range(2):
            acc = jnp.zeros((CONV_ROWS // 2, LANES), F32)
            for j in range(CONV_TAPS):
                start = r0 + (CONV_HALO - (CONV_TAPS - 1) + j + ph)
                acc = acc + ubuf_s[ci, pl.ds(start, CONV_ROWS // 2, stride=2), :] * cw_ref[j:j + 1, cs]
            cv_s[ci, pl.ds(r0 + ph, CONV_ROWS // 2, stride=2), :] = acc + cb_ref[:, cs]

    def conv_norm_piece(r0):
        rs = slice(r0, r0 + CONV_ROWS)
        cv = jnp.concatenate([cv_s[ci, rs, :] for ci in range(lane_tiles)], axis=-1)
        ub_s[rs, :] = _silu(_layer_norm(cv, clg_ref[...], clb_ref[...])).astype(BF16)

    col_pieces = [slice(c0, c0 + PROJ_COLS) for c0 in range(0, width, PROJ_COLS)]
    for cols in col_pieces:
        glu_piece(cols)
    vector_work = [functools.partial(conv_piece, ci, r0)
                   for ci in range(lane_tiles) for r0 in range(0, rows, CONV_ROWS)]
    vector_work += [functools.partial(conv_norm_piece, r0) for r0 in range(0, rows, CONV_ROWS)]
    segments = (q_piece, f_piece, v_piece, og_piece, ga_piece, gb_piece)
    for i, fn in enumerate(segments):
        for cols in col_pieces:
            fn(cols)
        for work in vector_work[len(vector_work) * i // len(segments):len(vector_work) * (i + 1) // len(segments)]:
            work()
    for ci in range(lane_tiles):
        ubuf_s[ci, 0:CONV_HALO, :] = ubuf_s[ci, rows:rows + CONV_HALO, :]

    n_block_levels = N_LEVELS - N_TABLE_LEVELS

    def intra_body(it, carry):
        for sub in range(INTRA_CHUNKS):
            c = it * INTRA_CHUNKS + sub
            rs = slice(c * CHUNK, (c + 1) * CHUNK)
            e_c = e_s.at[sub]
            e_c[...] = _dot(sums_ref[...], g_s[c])
            for h in range(heads):
                ls = slice(h * HG_DK, (h + 1) * HG_DK)
                q = q_s[rs, ls]
                k = k_s[rs, ls]
                qd_s[rs, ls] = q * jnp.exp2(e_c[0:CHUNK, ls]).astype(BF16)
                a = masks_ref[0] * _dot_nt(q, k)
                for l in range(N_LEVELS):
                    if l < n_block_levels:
                        e_l = jnp.exp2(_block_level_exponent(e_c, ls, CHUNK >> (l + 1))).astype(BF16)
                    else:
                        i = 1 + l - n_block_levels
                        e_l = jnp.exp2(e_c[i * CHUNK:(i + 1) * CHUNK, ls]).astype(BF16)
                    a = a + masks_ref[1 + l] * _dot_nt(q * e_l, k * e_l)
                a_s[c, h] = a.astype(BF16)
                b_end = e_c[CHUNK - 1:CHUNK, ls]
                ke_s[rs, ls] = k * jnp.exp2(b_end - e_c[0:CHUNK, ls]).astype(BF16)
                dec_s[c, 0:1, ls] = jnp.exp2(b_end)
        return carry

    for it in range(rows // (CHUNK * INTRA_CHUNKS)):
        intra_body(it, 0)

    for c in range(rows // CHUNK):
        rs = slice(c * CHUNK, (c + 1) * CHUNK)
        outs = []
        for h in range(heads):
            ls = slice(h * HG_DK, (h + 1) * HG_DK)
            st = st_s[h]
            v = v_s[rs, ls]
            outs.append(_dot_nt(qd_s[rs, ls], st.astype(BF16)) + _dot(a_s[c, h], v))
            st_s[h] = dec_s[c, 0:1, ls] * st + _dot_tn(v, ke_s[rs, ls])
        for h in range(heads):
            ls = slice(h * HG_DK, (h + 1) * HG_DK)
            o = outs[h]
            o = o * lax.rsqrt(jnp.mean(o * o, axis=-1, keepdims=True) + LN_EPS)
            ob_s[rs, ls] = (o * og_s[rs, ls]).astype(BF16)

    y_a = _dot(ob_s[...], w_hg_ref[...])
    y_b = _dot(ub_s[...], w_cv_ref[...])
    mixed = ga_s[...] * y_a + gb_s[...] * y_b
    y = _dot(mixed.astype(BF16), w_out_ref[...])
    out_ref[...] = _layer_norm(alpha * x_ref[...] + y, l1g_ref[...], l1b_ref[...])


def _ffn_kernel(alpha, hidden, h_ref, w_in_ref, w_out_ref, g_ref, b_ref, out_ref, act_s):
    hb = h_ref[...].astype(BF16)
    for c in range(hidden // FFN_COLS):
        gate = _dot(hb, w_in_ref[:, c * FFN_COLS:(c + 1) * FFN_COLS])
        up = _dot(hb, w_in_ref[:, hidden + c * FFN_COLS:hidden + (c + 1) * FFN_COLS])
        act_s[:, c * FFN_COLS:(c + 1) * FFN_COLS] = (_silu(gate) * up).astype(BF16)
    y = _dot(act_s[...], w_out_ref[...])
    out_ref[...] = _layer_norm(alpha * h_ref[...] + y, g_ref[...], b_ref[...])


def _resident(shape):
    return pl.BlockSpec(shape, lambda *_: (0,) * len(shape), pipeline_mode=pl.Buffered(1))


def _mixer_call(layer, alpha, x, w_in, lb_param, hgn, w_hg, cw, cb, clg, clb, w_cv, w_out, l1g, l1b):
    bsz, seq, d_model = x.shape
    heads = d_model // HG_DK
    rows = min(MIXER_ROWS, seq)
    assert seq % rows == 0 and rows % (CHUNK * INTRA_CHUNKS) == 0 and d_model % LANES == 0 and rows >= CONV_HALO
    assert N_TABLE_LEVELS <= N_LEVELS and (CHUNK >> (N_LEVELS - N_TABLE_LEVELS)) % SUBLANES == 0
    sums, masks = _decay_tables()
    sums = jnp.asarray(sums, BF16)
    masks = jnp.asarray(masks, F32)
    operands = (w_in, lb_param, sums, masks, hgn, w_hg, cw, cb, clg, clb, w_cv, w_out, l1g, l1b)
    row_block = pl.BlockSpec((None, rows, d_model), lambda b, s: (b, s, 0))
    act = lambda dt: pltpu.VMEM((rows, d_model), dt)
    return pl.pallas_call(
        functools.partial(_mixer_kernel, layer, alpha, d_model, heads),
        grid=(bsz, seq // rows),
        in_specs=[row_block] + [_resident(a.shape) for a in operands],
        out_specs=row_block,
        out_shape=jax.ShapeDtypeStruct(x.shape, F32),
        scratch_shapes=[
            pltpu.VMEM((heads, HG_DV, HG_DK), F32),
            pltpu.VMEM((d_model // LANES, rows + CONV_HALO, LANES), F32),
            act(BF16), act(BF16),
            pltpu.VMEM((rows // CHUNK, 2 * CHUNK, d_model), BF16),
            act(BF16),
            act(F32), act(F32), act(F32),
            pltpu.VMEM((INTRA_CHUNKS, (1 + N_TABLE_LEVELS) * CHUNK, d_model), F32),
            act(BF16),
            pltpu.VMEM((rows // CHUNK, heads, CHUNK, CHUNK), BF16),
            act(BF16),
            pltpu.VMEM((rows // CHUNK, SUBLANES, d_model), F32),
            act(BF16),
            pltpu.VMEM((d_model // LANES, rows, LANES), F32),
            act(BF16),
        ],
        compiler_params=pltpu.CompilerParams(
            dimension_semantics=("arbitrary", "arbitrary"),
            vmem_limit_bytes=V7X_VMEM_BYTES * 7 // 8),
        name="mixer",
    )(x, *operands)


def _ffn_call(alpha, h, w_in, w_out, g, b):
    n, d_model = h.shape
    hidden = w_out.shape[0]
    rows = min(FFN_ROWS, n)
    assert n % rows == 0 and hidden % FFN_COLS == 0
    row_block = pl.BlockSpec((rows, d_model), lambda i: (i, 0))
    return pl.pallas_call(
        functools.partial(_ffn_kernel, alpha, hidden),
        grid=(n // rows,),
        in_specs=[row_block, _resident(w_in.shape), _resident(w_out.shape), _resident(g.shape), _resident(b.shape)],
        out_specs=row_block,
        out_shape=jax.ShapeDtypeStruct(h.shape, F32),
        scratch_shapes=[pltpu.VMEM((rows, hidden), BF16)],
        compiler_params=pltpu.CompilerParams(
            dimension_semantics=("arbitrary",),
            vmem_limit_bytes=V7X_VMEM_BYTES * 7 // 8),
        name="ffn",
    )(h, w_in, w_out, g, b)


def kernel(x, w_in, lb_param, hg_norm_g, w_hg_out, conv_w, conv_b, conv_ln_g, conv_ln_b, w_conv_out, w_out,
           ln1_g, ln1_b, w_ffn_in, w_ffn_out, ln2_g, ln2_b):
    bsz, seq, d_model = x.shape
    depth = w_in.shape[0]
    alpha = (2 * depth) ** 0.25
    row = lambda a: a.reshape(1, -1).astype(F32)
    h = x
    for l in range(depth):
        h = _mixer_call(l, alpha, h, w_in[l].astype(BF16), lb_param.astype(F32), row(hg_norm_g[l]),
                        w_hg_out[l].astype(BF16), conv_w[l].astype(F32), row(conv_b[l]), row(conv_ln_g[l]),
                        row(conv_ln_b[l]), w_conv_out[l].astype(BF16), w_out[l].astype(BF16),
                        row(ln1_g[l]), row(ln1_b[l]))
        h = _ffn_call(alpha, h.reshape(bsz * seq, d_model), w_ffn_in[l].astype(BF16), w_ffn_out[l].astype(BF16),
                      row(ln2_g[l]), row(ln2_b[l])).reshape(bsz, seq, d_model)
    return h
```

```python
import functools

import numpy as np
import jax
import jax.numpy as jnp
from jax import lax
from jax.experimental import pallas as pl
from jax.experimental.pallas import tpu as pltpu

HG_DK = 128
HG_DV = 128
LANES = 128
SUBLANES = 8
CONV_TAPS = 31
LN_EPS = 1e-5
CONV_HALO = 32
CHUNK = 64
N_LEVELS = CHUNK.bit_length() - 1
N_TABLE_LEVELS = 3
INTRA_CHUNKS = 4
MIXER_ROWS = 256
PROJ_COLS = 256
CONV_ROWS = 64
FFN_ROWS = 1024
FFN_COLS = 256
V7X_VMEM_BYTES = 64 * 1024 * 1024

BF16 = jnp.bfloat16
F32 = jnp.float32


def _decay_tables():
    c = CHUNK
    sums = np.zeros((1 + N_TABLE_LEVELS, c, c), np.float32)
    masks = np.zeros((1 + N_LEVELS, c, c), np.float32)
    masks[0] = np.eye(c)
    for r in range(c):
        sums[0, r, :r + 1] = 1.0
    for l in range(N_LEVELS):
        m = c >> (l + 1)
        i = l - (N_LEVELS - N_TABLE_LEVELS)
        for r in range(c):
            a = (r // (2 * m)) * 2 * m
            if r >= a + m:
                masks[1 + l, r, a:a + m] = 1.0
                if i >= 0:
                    sums[1 + i, r, a + m:r + 1] = 1.0
            elif i >= 0:
                sums[1 + i, r, r + 1:a + m] = 1.0
    sums = sums.reshape((1 + N_TABLE_LEVELS) * c, c)
    return np.concatenate([sums, sums], axis=1), masks


def _sigmoid(x):
    return 1.0 / (1.0 + jnp.exp(-x))


def _silu(x):
    return x * _sigmoid(x)


def _layer_norm(x, g, b):
    mu = jnp.mean(x, axis=-1, keepdims=True)
    xc = x - mu
    var = jnp.mean(xc * xc, axis=-1, keepdims=True)
    return xc * lax.rsqrt(var + LN_EPS) * g + b


def _dot(a, b):
    return jnp.dot(a, b, preferred_element_type=F32)


def _dot_nt(a, b):
    return lax.dot_general(a, b, (((1,), (1,)), ((), ())), preferred_element_type=F32)


def _dot_tn(a, b):
    return lax.dot_general(a, b, (((0,), (0,)), ((), ())), preferred_element_type=F32)


def _block_level_exponent(b_ref, ls, m):
    parts = []
    for a in range(0, CHUNK, 2 * m):
        beta = b_ref[a + m - 1:a + m, ls]
        parts.append(beta - b_ref[a:a + m, ls])
        parts.append(b_ref[a + m:a + 2 * m, ls] - beta)
    return jnp.concatenate(parts, axis=0)


def _mixer_kernel(layer, alpha, d_model, heads,
                  x_ref, w_in_ref, lbp_ref, sums_ref, masks_ref, hgn_ref, w_hg_ref,
                  cw_ref, cb_ref, clg_ref, clb_ref, w_cv_ref, w_out_ref, l1g_ref, l1b_ref,
                  out_ref,
                  st_s, ubuf_s, q_s, k_s, g_s, v_s, og_s, ga_s, gb_s, e_s, qd_s, a_s, ke_s, dec_s, ob_s, cv_s, ub_s):
    rows = x_ref.shape[0]
    width = heads * HG_DK
    lane_tiles = d_model // LANES

    @pl.when(pl.program_id(1) == 0)
    def _():
        st_s[...] = jnp.zeros_like(st_s)
        ubuf_s[:, 0:CONV_HALO, :] = jnp.zeros((lane_tiles, CONV_HALO, LANES), F32)

    lbp = lbp_ref[...]
    lbe = jnp.exp(lbp - jnp.max(lbp, axis=0, keepdims=True))
    lb = jnp.sum(lbe[0:layer + 1, :], axis=0, keepdims=True) / jnp.sum(lbe, axis=0, keepdims=True)

    xb = x_ref[...].astype(BF16)

    def proj(i, cols):
        return _dot(xb, w_in_ref[:, i * width + cols.start:i * width + cols.stop])

    def glu_piece(cols):
        u = proj(4, cols) * _sigmoid(proj(5, cols))
        for ci in range(cols.start // LANES, cols.stop // LANES):
            ubuf_s[ci, CONV_HALO:CONV_HALO + rows, :] = u[:, ci * LANES - cols.start:(ci + 1) * LANES - cols.start]

    def q_piece(cols):
        q_s[:, cols] = (_silu(proj(0, cols)) * (HG_DK ** -0.5)).astype(BF16)

    def f_piece(cols):
        f = lb[:, cols] + (1.0 - lb[:, cols]) * _sigmoid(proj(1, cols))
        k_s[:, cols] = (1.0 - f).astype(BF16)
        g = jnp.log2(f)
        ghi = g.astype(BF16)
        glo = (g - ghi.astype(F32)).astype(BF16)
        for c in range(rows // CHUNK):
            g_s[c, 0:CHUNK, cols] = ghi[c * CHUNK:(c + 1) * CHUNK, :]
            g_s[c, CHUNK:2 * CHUNK, cols] = glo[c * CHUNK:(c + 1) * CHUNK, :]

    def v_piece(cols):
        v_s[:, cols] = proj(2, cols).astype(BF16)

    def og_piece(cols):
        gain = jnp.concatenate([hgn_ref[...]] * (PROJ_COLS // HG_DV), axis=-1)
        og_s[:, cols] = _silu(proj(3, cols)) * gain

    def ga_piece(cols):
        ga_s[:, cols] = _sigmoid(proj(6, cols))

    def gb_piece(cols):
        gb_s[:, cols] = _sigmoid(proj(7, cols))

    def conv_piece(ci, r0):
        cs = slice(ci * LANES, (ci + 1) * LANES)
        for ph in range(2):
            acc = jnp.zeros((CONV_ROWS // 2, LANES), F32)
            for j in range(CONV_TAPS):
                start = r0 + (CONV_HALO - (CONV_TAPS - 1) + j + ph)
                acc = acc + ubuf_s[ci, pl.ds(start, CONV_ROWS // 2, stride=2), :] * cw_ref[j:j + 1, cs]
            cv_s[ci, pl.ds(r0 + ph, CONV_ROWS // 2, stride=2), :] = acc + cb_ref[:, cs]

    def conv_norm_piece(r0):
        rs = slice(r0, r0 + CONV_ROWS)
        cv = jnp.concatenate([cv_s[ci, rs, :] for ci in range(lane_tiles)], axis=-1)
        ub_s[rs, :] = _silu(_layer_norm(cv, clg_ref[...], clb_ref[...])).astype(BF16)

    col_pieces = [slice(c0, c0 + PROJ_COLS) for c0 in range(0, width, PROJ_COLS)]
    for cols in col_pieces:
        glu_piece(cols)
    vector_work = [functools.partial(conv_piece, ci, r0)
                   for ci in range(lane_tiles) for r0 in range(0, rows, CONV_ROWS)]
    vector_work += [functools.partial(conv_norm_piece, r0) for r0 in range(0, rows, CONV_ROWS)]
    segments = (q_piece, f_piece, v_piece, og_piece, ga_piece, gb_piece)
    for i, fn in enumerate(segments):
        for cols in col_pieces:
            fn(cols)
        for work in vector_work[len(vector_work) * i // len(segments):len(vector_work) * (i + 1) // len(segments)]:
            work()
    for ci in range(lane_tiles):
        ubuf_s[ci, 0:CONV_HALO, :] = ubuf_s[ci, rows:rows + CONV_HALO, :]

    n_block_levels = N_LEVELS - N_TABLE_LEVELS

    def intra_body(it, carry):
        for sub in range(INTRA_CHUNKS):
            c = it * INTRA_CHUNKS + sub
            rs = slice(c * CHUNK, (c + 1) * CHUNK)
            e_c = e_s.at[sub]
            e_c[...] = _dot(sums_ref[...], g_s[c])
            for h in range(heads):
                ls = slice(h * HG_DK, (h + 1) * HG_DK)
                q = q_s[rs, ls]
                k = k_s[rs, ls]
                qd_s[rs, ls] = q * jnp.exp2(e_c[0:CHUNK, ls]).astype(BF16)
                a = masks_ref[0] * _dot_nt(q, k)
                for l in range(N_LEVELS):
                    if l < n_block_levels:
                        e_l = jnp.exp2(_block_level_exponent(e_c, ls, CHUNK >> (l + 1))).astype(BF16)
                    else:
                        i = 1 + l - n_block_levels
                        e_l = jnp.exp2(e_c[i * CHUNK:(i + 1) * CHUNK, ls]).astype(BF16)
                    a = a + masks_ref[1 + l] * _dot_nt(q * e_l, k * e_l)
                a_s[c, h] = a.astype(BF16)
                b_end = e_c[CHUNK - 1:CHUNK, ls]
                ke_s[rs, ls] = k * jnp.exp2(b_end - e_c[0:CHUNK, ls]).astype(BF16)
                dec_s[c, 0:1, ls] = jnp.exp2(b_end)
        return carry

    for it in range(rows // (CHUNK * INTRA_CHUNKS)):
        intra_body(it, 0)

    for c in range(rows // CHUNK):
        rs = slice(c * CHUNK, (c + 1) * CHUNK)
        outs = []
        for h in range(heads):
            ls = slice(h * HG_DK, (h + 1) * HG_DK)
            st = st_s[h]
            v = v_s[rs, ls]
            outs.append(_dot_nt(qd_s[rs, ls], st.astype(BF16)) + _dot(a_s[c, h], v))
            st_s[h] = dec_s[c, 0:1, ls] * st + _dot_tn(v, ke_s[rs, ls])
        for h in range(heads):
            ls = slice(h * HG_DK, (h + 1) * HG_DK)
            o = outs[h]
            o = o * lax.rsqrt(jnp.mean(o * o, axis=-1, keepdims=True) + LN_EPS)
            ob_s[rs, ls] = (o * og_s[rs, ls]).astype(BF16)

    y_a = _dot(ob_s[...], w_hg_ref[...])
    y_b = _dot(ub_s[...], w_cv_ref[...])
    mixed = ga_s[...] * y_a + gb_s[...] * y_b
    y = _dot(mixed.astype(BF16), w_out_ref[...])
    out_ref[...] = _layer_norm(alpha * x_ref[...] + y, l1g_ref[...], l1b_ref[...])


def _ffn_kernel(alpha, hidden, h_ref, w_in_ref, w_out_ref, g_ref, b_ref, out_ref, act_s):
    hb = h_ref[...].astype(BF16)
    for c in range(hidden // FFN_COLS):
        gate = _dot(hb, w_in_ref[:, c * FFN_COLS:(c + 1) * FFN_COLS])
        up = _dot(hb, w_in_ref[:, hidden + c * FFN_COLS:hidden + (c + 1) * FFN_COLS])
        act_s[:, c * FFN_COLS:(c + 1) * FFN_COLS] = (_silu(gate) * up).astype(BF16)
    y = _dot(act_s[...], w_out_ref[...])
    out_ref[...] = _layer_norm(alpha * h_ref[...] + y, g_ref[...], b_ref[...])


def _resident(shape):
    return pl.BlockSpec(shape, lambda *_: (0,) * len(shape), pipeline_mode=pl.Buffered(1))


def _mixer_call(layer, alpha, x, w_in, lb_param, hgn, w_hg, cw, cb, clg, clb, w_cv, w_out, l1g, l1b):
    bsz, seq, d_model = x.shape
    heads = d_model // HG_DK
    rows = min(MIXER_ROWS, seq)
    assert seq % rows == 0 and rows % (CHUNK * INTRA_CHUNKS) == 0 and d_model % LANES == 0 and rows >= CONV_HALO
    assert N_TABLE_LEVELS <= N_LEVELS and (CHUNK >> (N_LEVELS - N_TABLE_LEVELS)) % SUBLANES == 0
    sums, masks = _decay_tables()
    sums = jnp.asarray(sums, BF16)
    masks = jnp.asarray(masks, F32)
    operands = (w_in, lb_param, sums, masks, hgn, w_hg, cw, cb, clg, clb, w_cv, w_out, l1g, l1b)
    row_block = pl.BlockSpec((None, rows, d_model), lambda b, s: (b, s, 0))
    act = lambda dt: pltpu.VMEM((rows, d_model), dt)
    return pl.pallas_call(
        functools.partial(_mixer_kernel, layer, alpha, d_model, heads),
        grid=(bsz, seq // rows),
        in_specs=[row_block] + [_resident(a.shape) for a in operands],
        out_specs=row_block,
        out_shape=jax.ShapeDtypeStruct(x.shape, F32),
        scratch_shapes=[
            pltpu.VMEM((heads, HG_DV, HG_DK), F32),
            pltpu.VMEM((d_model // LANES, rows + CONV_HALO, LANES), F32),
            act(BF16), act(BF16),
            pltpu.VMEM((rows // CHUNK, 2 * CHUNK, d_model), BF16),
            act(BF16),
            act(F32), act(F32), act(F32),
            pltpu.VMEM((INTRA_CHUNKS, (1 + N_TABLE_LEVELS) * CHUNK, d_model), F32),
            act(BF16),
            pltpu.VMEM((rows // CHUNK, heads, CHUNK, CHUNK), BF16),
            act(BF16),
            pltpu.VMEM((rows // CHUNK, SUBLANES, d_model), F32),
            act(BF16),
            pltpu.VMEM((d_model // LANES, rows, LANES), F32),
            act(BF16),
        ],
        compiler_params=pltpu.CompilerParams(
            dimension_semantics=("arbitrary", "arbitrary"),
            vmem_limit_bytes=V7X_VMEM_BYTES * 7 // 8),
        name="mixer",
    )(x, *operands)


def _ffn_call(alpha, h, w_in, w_out, g, b):
    n, d_model = h.shape
    hidden = w_out.shape[0]
    rows = min(FFN_ROWS, n)
    assert n % rows == 0 and hidden % FFN_COLS == 0
    row_block = pl.BlockSpec((rows, d_model), lambda i: (i, 0))
    return pl.pallas_call(
        functools.partial(_ffn_kernel, alpha, hidden),
        grid=(n // rows,),
        in_specs=[row_block, _resident(w_in.shape), _resident(w_out.shape), _resident(g.shape), _resident(b.shape)],
        out_specs=row_block,
        out_shape=jax.ShapeDtypeStruct(h.shape, F32),
        scratch_shapes=[pltpu.VMEM((rows, hidden), BF16)],
        compiler_params=pltpu.CompilerParams(
            dimension_semantics=("arbitrary",),
            allow_input_fusion=[False, True, True, False, False],
            vmem_limit_bytes=V7X_VMEM_BYTES * 7 // 8),
        name="ffn",
    )(h, w_in, w_out, g, b)


def kernel(x, w_in, lb_param, hg_norm_g, w_hg_out, conv_w, conv_b, conv_ln_g, conv_ln_b, w_conv_out, w_out,
           ln1_g, ln1_b, w_ffn_in, w_ffn_out, ln2_g, ln2_b):
    bsz, seq, d_model = x.shape
    depth = w_in.shape[0]
    alpha = (2 * depth) ** 0.25
    row = lambda a: a.reshape(1, -1).astype(F32)
    h = x
    for l in range(depth):
        h = _mixer_call(l, alpha, h, w_in[l].astype(BF16), lb_param.astype(F32), row(hg_norm_g[l]),
                        w_hg_out[l].astype(BF16), conv_w[l].astype(F32), row(conv_b[l]), row(conv_ln_g[l]),
                        row(conv_ln_b[l]), w_conv_out[l].astype(BF16), w_out[l].astype(BF16),
                        row(ln1_g[l]), row(ln1_b[l]))
        h = _ffn_call(alpha, h.reshape(bsz * seq, d_model), w_ffn_in[l].astype(BF16), w_ffn_out[l].astype(BF16),
                      row(ln2_g[l]), row(ln2_b[l])).reshape(bsz, seq, d_model)
    return h
```
